```python
import jax, jax.numpy as jnp
from jax import lax
import numpy as np

D_MODEL = 1024
BATCH = 8
SEQ = 4096
DEPTH = 2

W_A = 512
A_GROUPS = 8
A_CONV = 3
W_B = 1024
LRU_HEADS = 8
LRU_HEAD_DIM = W_B // LRU_HEADS
B_CONV = 4
LRU_C = 8.0
W_C = 512
POOL_WINDOWS = (2, 4, 8, 16)
POOL_GROUPS = len(POOL_WINDOWS)
POOL_GROUP_DIM = W_C // POOL_GROUPS
N_BRANCH = 3
SPLITS = [W_A, 2 * W_A, 3 * W_A, 3 * W_A + W_B, 3 * W_A + 2 * W_B,
          3 * W_A + 2 * W_B + W_C, 3 * W_A + 2 * W_B + W_C + D_MODEL,
          3 * W_A + 2 * W_B + W_C + 2 * D_MODEL]
IN_COLS = 3 * W_A + 2 * W_B + W_C + N_BRANCH * D_MODEL
N_GROUPS = 4
EXPERTS_PER_GROUP = 4
N_EXPERTS = N_GROUPS * EXPERTS_PER_GROUP
TOP_K = 2
D_EXPERT = 512
EPS = 1e-6

kernel_name = "hybrid_conv_lru_pool_hmoe"


def rmsnorm(x, g):
    xf = x.astype(jnp.float32)
    y = xf * lax.rsqrt(jnp.mean(xf * xf, axis=-1, keepdims=True) + EPS)
    return (y * g.astype(jnp.float32)).astype(x.dtype)


def causal_dwconv(x, w):
    k, c = w.shape
    return lax.conv_general_dilated(
        x, w[:, None, :].astype(x.dtype), window_strides=(1,), padding=[(k - 1, 0)],
        dimension_numbers=("NWC", "WIO", "NWC"), feature_group_count=c)


def rg_lru(x, w_a, b_a, w_x, b_x, lam):
    bsz, s, w = x.shape
    xh = x.reshape(bsz, s, LRU_HEADS, LRU_HEAD_DIM)
    r = jax.nn.sigmoid(jnp.einsum("bshi,hij->bshj", xh, w_a) + b_a).reshape(bsz, s, w)
    i = jax.nn.sigmoid(jnp.einsum("bshi,hij->bshj", xh, w_x) + b_x).reshape(bsz, s, w)
    log_a = -LRU_C * r.astype(jnp.float32) * jax.nn.softplus(-lam.astype(jnp.float32))
    a = jnp.exp(log_a)
    b = jnp.sqrt(-jnp.expm1(2.0 * log_a)) * (i * x).astype(jnp.float32)

    def combine(left, right):
        a_l, b_l = left
        a_r, b_r = right
        return a_r * a_l, a_r * b_l + b_r

    _, h = lax.associative_scan(combine, (a, b), axis=1)
    return h.astype(x.dtype)


def multiscale_pool(u):
    s = u.shape[1]
    cs = jnp.cumsum(u.astype(jnp.float32), axis=1)
    t = jnp.arange(1, s + 1, dtype=jnp.float32)
    outs = []
    for g, w in enumerate(POOL_WINDOWS):
        c = cs[..., g * POOL_GROUP_DIM:(g + 1) * POOL_GROUP_DIM]
        lag = jnp.pad(c, ((0, 0), (w, 0), (0, 0)))[:, :s]
        cnt = jnp.minimum(t, float(w))[None, :, None]
        outs.append((c - lag) / cnt)
    pooled = jnp.concatenate(outs, axis=-1)
    return (pooled - u.astype(jnp.float32)).astype(u.dtype)


def mixer_sublayer(h, w_in, conv_a_w, w_a_out, conv_b_w, conv_b_b, lru_wa, lru_ba,
                   lru_wx, lru_bx, lru_lambda, w_b_out, pool_w, pool_scale, w_c_out, w_o):
    bsz, s, _ = h.shape
    z = h @ w_in
    a_b, a_c, a_x, b_x, b_y, c_u, g_a, g_b, g_c = jnp.split(z, SPLITS, axis=-1)
    y_a = a_b * causal_dwconv(a_c * a_x, conv_a_w)
    xb = causal_dwconv(b_x, conv_b_w) + conv_b_b
    y_b = rg_lru(xb, lru_wa, lru_ba, lru_wx, lru_bx, lru_lambda) * jax.nn.gelu(b_y)
    pc = multiscale_pool(c_u).reshape(bsz, s, POOL_GROUPS, POOL_GROUP_DIM)
    y_c = jnp.einsum("bsgi,gij->bsgj", pc, pool_w).reshape(bsz, s, W_C) * pool_scale
    m = (jax.nn.sigmoid(g_a) * (y_a @ w_a_out)
         + jax.nn.sigmoid(g_b) * (y_b @ w_b_out)
         + jax.nn.sigmoid(g_c) * (y_c @ w_c_out))
    return m @ w_o


def hier_moe(h, w_grp, b_grp, w_exp, b_exp, w1, w3, w2):
    bsz, s, d = h.shape
    t = h.reshape(-1, d)
    n_tok = t.shape[0]
    grp_prob = jax.nn.softmax((t @ w_grp + b_grp).astype(jnp.float32), axis=-1)
    g_p, g_idx = lax.top_k(grp_prob, 1)
    exp_logits = (t @ w_exp + b_exp).astype(jnp.float32).reshape(n_tok, N_GROUPS, EXPERTS_PER_GROUP)
    sel = jnp.take_along_axis(exp_logits, g_idx[:, :, None], axis=1)[:, 0]
    e_val, e_idx = lax.top_k(sel, TOP_K)
    e_w = jax.nn.softmax(e_val, axis=-1) * g_p
    within = jnp.sum(jax.nn.one_hot(e_idx, EXPERTS_PER_GROUP, dtype=jnp.float32) * e_w[..., None], axis=1)
    gate = (jax.nn.one_hot(g_idx[:, 0], N_GROUPS, dtype=jnp.float32)[:, :, None]
            * within[:, None, :]).reshape(n_tok, N_EXPERTS).astype(h.dtype)
    out = jnp.zeros_like(t)
    for e in range(N_EXPERTS):
        hid = jax.nn.silu(t @ w1[e]) * (t @ w3[e])
        out = out + gate[:, e:e + 1] * (hid @ w2[e])
    return out.reshape(bsz, s, d)


def setup_inputs(seed: int = 0) -> dict:
    key = jax.random.key(seed)
    ks = jax.random.split(key, 32)

    def nrm(k, shape, scale):
        return jax.random.normal(k, shape, jnp.float32) * scale

    L, D = DEPTH, D_MODEL
    u = jax.random.uniform(ks[12], (L, W_B), jnp.float32, minval=0.9, maxval=0.999)
    sig = u ** (1.0 / LRU_C)
    lam = jnp.log(sig) - jnp.log1p(-sig)
    return {
        "x": nrm(ks[0], (BATCH, SEQ, D), 1.0),
        "norm1_g": 1.0 + nrm(ks[1], (L, D), 0.02),
        "w_in": nrm(ks[2], (L, D, IN_COLS), D ** -0.5),
        "conv_a_w": nrm(ks[3], (L, A_CONV, W_A), A_CONV ** -0.5),
        "w_a_out": nrm(ks[4], (L, W_A, D), W_A ** -0.5),
        "conv_b_w": nrm(ks[5], (L, B_CONV, W_B), B_CONV ** -0.5),
        "conv_b_b": nrm(ks[6], (L, W_B), 0.01),
        "lru_wa": nrm(ks[7], (L, LRU_HEADS, LRU_HEAD_DIM, LRU_HEAD_DIM), LRU_HEAD_DIM ** -0.5),
        "lru_ba": nrm(ks[8], (L, LRU_HEADS, LRU_HEAD_DIM), 0.01),
        "lru_wx": nrm(ks[9], (L, LRU_HEADS, LRU_HEAD_DIM, LRU_HEAD_DIM), LRU_HEAD_DIM ** -0.5),
        "lru_bx": nrm(ks[10], (L, LRU_HEADS, LRU_HEAD_DIM), 0.01),
        "lru_lambda": lam,
        "w_b_out": nrm(ks[11], (L, W_B, D), W_B ** -0.5),
        "pool_w": nrm(ks[13], (L, POOL_GROUPS, POOL_GROUP_DIM, POOL_GROUP_DIM), POOL_GROUP_DIM ** -0.5),
        "pool_scale": 1.0 + nrm(ks[14], (L, W_C), 0.02),
        "w_c_out": nrm(ks[15], (L, W_C, D), W_C ** -0.5),
        "w_o": nrm(ks[16], (L, D, D), D ** -0.5),
        "norm2_g": 1.0 + nrm(ks[17], (L, D), 0.02),
        "router_grp_w": nrm(ks[18], (L, D, N_GROUPS), D ** -0.5),
        "router_grp_b": nrm(ks[19], (L, N_GROUPS), 0.01),
        "router_exp_w": nrm(ks[20], (L, D, N_EXPERTS), D ** -0.5),
        "router_exp_b": nrm(ks[21], (L, N_EXPERTS), 0.01),
        "moe_w1": nrm(ks[22], (L, N_EXPERTS, D, D_EXPERT), D ** -0.5),
        "moe_w3": nrm(ks[23], (L, N_EXPERTS, D, D_EXPERT), D ** -0.5),
        "moe_w2": nrm(ks[24], (L, N_EXPERTS, D_EXPERT, D), D_EXPERT ** -0.5),
        "final_g": 1.0 + nrm(ks[25], (D,), 0.02),
    }


def reference(x, norm1_g, w_in, conv_a_w, w_a_out, conv_b_w, conv_b_b, lru_wa, lru_ba,
              lru_wx, lru_bx, lru_lambda, w_b_out, pool_w, pool_scale, w_c_out, w_o,
              norm2_g, router_grp_w, router_grp_b, router_exp_w, router_exp_b,
              moe_w1, moe_w3, moe_w2, final_g):
    for l in range(DEPTH):
        h = rmsnorm(x, norm1_g[l])
        x = x + mixer_sublayer(h, w_in[l], conv_a_w[l], w_a_out[l], conv_b_w[l], conv_b_b[l],
                               lru_wa[l], lru_ba[l], lru_wx[l], lru_bx[l], lru_lambda[l],
                               w_b_out[l], pool_w[l], pool_scale[l], w_c_out[l], w_o[l])
        h = rmsnorm(x, norm2_g[l])
        x = x + hier_moe(h, router_grp_w[l], router_grp_b[l], router_exp_w[l], router_exp_b[l],
                         moe_w1[l], moe_w3[l], moe_w2[l])
    return rmsnorm(x, final_g)
```

```python
import functools

import jax
import jax.numpy as jnp
from jax import lax
from jax.experimental import pallas as pl
from jax.experimental.pallas import tpu as pltpu

LANES = 128
SUBLANES = 8
VMEM_LIMIT_BYTES = 56 * 1024 * 1024

EPS = 1e-6
LRU_C = 8.0
POOL_WINDOWS = (2, 4, 8, 16)
A_CONV = 3
B_CONV = 4
EXPERTS_PER_GROUP = 4
PAIRS = ((0, 1), (0, 2), (0, 3), (1, 2), (1, 3), (2, 3))
N_PAIRS = len(PAIRS)

MIXER_TIMESTEPS = 64
MOE_TILE = 256
AUG = LANES
CNT_ROWS = 32
NEG_BIG = -1e30


def _dot(a, b):
    return jnp.dot(a, b, preferred_element_type=jnp.float32)


def _rmsnorm(x, g):
    return x * lax.rsqrt(jnp.mean(x * x, axis=-1, keepdims=True) + EPS) * g


def _shifted(buf, val, halo, shifts):
    rows = val.shape[0]
    buf[halo:halo + rows, :] = val
    outs = [buf[halo - s:halo - s + rows, :] for s in shifts]
    buf[0:halo, :] = buf[rows:rows + halo, :]
    return outs


def _mixer_body(dims, x_ref, g1_ref, w_in_ref, caw_ref, wao_ref, cbw_ref, cbb_ref, lwa_ref, lba_ref,
                lwx_ref, lbx_ref, lam_ref, wbo_ref, pw_ref, ps_ref, wco_ref, wo_ref, g2_ref,
                wrh_ref, wrl_ref, br_ref,
                xa_ref, cls_ref, cnt_ref,
                buf_a, buf_b, buf_c1, buf_c2, buf_c4, buf_c8, a_scr, b_scr, h_state):
    bsz, tt, d, wa, wb, wc, heads, n_groups = dims
    rows = bsz * tt
    hd = wb // heads
    gd = wc // len(POOL_WINDOWS)
    i = pl.program_id(0)

    @pl.when(i == 0)
    def _():
        buf_a[0:(A_CONV - 1) * bsz, :] = jnp.zeros(((A_CONV - 1) * bsz, wa), jnp.float32)
        buf_b[0:(B_CONV - 1) * bsz, :] = jnp.zeros(((B_CONV - 1) * bsz, wb), jnp.float32)
        buf_c1[0:bsz, :] = jnp.zeros((bsz, wc), jnp.float32)
        buf_c2[0:2 * bsz, :] = jnp.zeros((2 * bsz, wc - gd), jnp.float32)
        buf_c4[0:4 * bsz, :] = jnp.zeros((4 * bsz, wc - 2 * gd), jnp.float32)
        buf_c8[0:8 * bsz, :] = jnp.zeros((8 * bsz, wc - 3 * gd), jnp.float32)
        h_state[...] = jnp.zeros((bsz, wb), jnp.float32)

    x = x_ref[...]
    h = _rmsnorm(x, g1_ref[...]).astype(jnp.bfloat16)

    o_b = 3 * wa
    o_c = o_b + 2 * wb
    o_g = o_c + wc

    z_a = _dot(h, w_in_ref[:, 0:o_b])
    p = z_a[:, wa:2 * wa] * z_a[:, 2 * wa:3 * wa]
    p1, p2 = _shifted(buf_a, p, (A_CONV - 1) * bsz, (bsz, 2 * bsz))
    caw = caw_ref[...]
    y_a = z_a[:, 0:wa] * (caw[2:3, :] * p + caw[1:2, :] * p1 + caw[0:1, :] * p2)
    m = jax.nn.sigmoid(_dot(h, w_in_ref[:, o_g:o_g + d])) * _dot(y_a.astype(jnp.bfloat16), wao_ref[...])

    z_b = _dot(h, w_in_ref[:, o_b:o_c])
    b_x = z_b[:, 0:wb]
    x1, x2, x3 = _shifted(buf_b, b_x, (B_CONV - 1) * bsz, (bsz, 2 * bsz, 3 * bsz))
    cbw = cbw_ref[...]
    xb = cbw[3:4, :] * b_x + cbw[2:3, :] * x1 + cbw[1:2, :] * x2 + cbw[0:1, :] * x3 + cbb_ref[...]
    xb16 = xb.astype(jnp.bfloat16)
    pre_r = jnp.concatenate([_dot(xb16[:, k * hd:(k + 1) * hd], lwa_ref[k]) for k in range(heads)], axis=-1)
    pre_i = jnp.concatenate([_dot(xb16[:, k * hd:(k + 1) * hd], lwx_ref[k]) for k in range(heads)], axis=-1)
    r = jax.nn.sigmoid(pre_r + lba_ref[...])
    ig = jax.nn.sigmoid(pre_i + lbx_ref[...])
    lam = lam_ref[...]
    softplus_neg_lam = jnp.maximum(-lam, 0.0) + jnp.log1p(jnp.exp(-jnp.abs(lam)))
    log_a = (-LRU_C * softplus_neg_lam) * r
    a = jnp.exp(log_a)
    a_scr[...] = a
    b_scr[...] = jnp.sqrt(1.0 - a * a) * (ig * xb)

    def scan_step(t, carry):
        sl = pl.ds(pl.multiple_of(t * bsz, bsz), bsz)
        carry = a_scr[sl, :] * carry + b_scr[sl, :]
        b_scr[sl, :] = carry
        return carry

    h_state[...] = lax.fori_loop(0, tt, scan_step, h_state[...])
    y_b = b_scr[...] * jax.nn.gelu(z_b[:, wb:2 * wb])
    m = m + jax.nn.sigmoid(_dot(h, w_in_ref[:, o_g + d:o_g + 2 * d])) * _dot(y_b.astype(jnp.bfloat16), wbo_ref[...])

    u = _dot(h, w_in_ref[:, o_c:o_g])
    (u1,) = _shifted(buf_c1, u, bsz, (bsz,))
    s2 = u + u1
    s2r = s2[:, gd:]
    (s2s,) = _shifted(buf_c2, s2r, 2 * bsz, (2 * bsz,))
    s4 = s2r + s2s
    s4r = s4[:, gd:]
    (s4s,) = _shifted(buf_c4, s4r, 4 * bsz, (4 * bsz,))
    s8 = s4r + s4s
    s8r = s8[:, gd:]
    (s8s,) = _shifted(buf_c8, s8r, 8 * bsz, (8 * bsz,))
    s16 = s8r + s8s
    sums = (s2[:, 0:gd], s4[:, 0:gd], s8[:, 0:gd], s16)
    t_abs = (lax.broadcasted_iota(jnp.int32, (rows, gd), 0) // bsz + i * tt + 1).astype(jnp.float32)
    ps = ps_ref[...]
    y_c = []
    for g, w in enumerate(POOL_WINDOWS):
        pooled = sums[g] / jnp.minimum(t_abs, float(w))
        pc = (pooled - u[:, g * gd:(g + 1) * gd]).astype(jnp.bfloat16)
        y_c.append(_dot(pc, pw_ref[g]) * ps[:, g * gd:(g + 1) * gd])
    y_c = jnp.concatenate(y_c, axis=-1)
    m = m + jax.nn.sigmoid(_dot(h, w_in_ref[:, o_g + 2 * d:o_g + 3 * d])) * _dot(y_c.astype(jnp.bfloat16), wco_ref[...])

    x_new = x + _dot(m.astype(jnp.bfloat16), wo_ref[...])
    xa_ref[:, 0:d] = x_new

    h2 = _rmsnorm(x_new, g2_ref[...])
    h2_hi = h2.astype(jnp.bfloat16)
    h2_lo = (h2 - h2_hi.astype(jnp.float32)).astype(jnp.bfloat16)
    logits = (_dot(h2_hi, wrh_ref[...]) + _dot(h2_hi, wrl_ref[...]) + _dot(h2_lo, wrh_ref[...])) + br_ref[...]
    lane = lax.broadcasted_iota(jnp.int32, (rows, LANES), 1).astype(jnp.float32)
    is_grp = lane < n_groups
    gl = jnp.where(is_grp, logits, NEG_BIG)
    gmax = jnp.max(gl, axis=-1, keepdims=True)
    gidx = jnp.min(jnp.where(gl == gmax, lane, float(LANES)), axis=-1, keepdims=True)
    g_p = 1.0 / jnp.sum(jnp.where(is_grp, jnp.exp(gl - gmax), 0.0), axis=-1, keepdims=True)
    e_base = n_groups + EXPERTS_PER_GROUP * gidx
    el = jnp.where((lane >= e_base) & (lane < e_base + EXPERTS_PER_GROUP), logits, NEG_BIG)
    m1 = jnp.max(el, axis=-1, keepdims=True)
    i1 = jnp.min(jnp.where(el == m1, lane, float(LANES)), axis=-1, keepdims=True)
    el2 = jnp.where(lane == i1, NEG_BIG, el)
    m2 = jnp.max(el2, axis=-1, keepdims=True)
    i2 = jnp.min(jnp.where(el2 == m2, lane, float(LANES)), axis=-1, keepdims=True)
    e21 = jnp.exp(m2 - m1)
    w1 = g_p / (1.0 + e21)
    w2 = g_p * e21 / (1.0 + e21)
    first_is_lo = i1 < i2
    loc_lo = jnp.where(first_is_lo, i1, i2) - e_base
    loc_hi = jnp.where(first_is_lo, i2, i1) - e_base
    pair = jnp.where(loc_lo == 0.0, loc_hi - 1.0, jnp.where(loc_lo == 1.0, loc_hi + 1.0, float(N_PAIRS - 1)))
    cls = gidx * float(N_PAIRS) + pair
    aug = jnp.where(lane == 0.0, jnp.where(first_is_lo, w1, w2),
                    jnp.where(lane == 1.0, jnp.where(first_is_lo, w2, w1),
                              jnp.where(lane == 2.0, cls, 0.0)))
    xa_ref[:, d:d + AUG] = aug
    cls_row = aug.T[2:3, :].astype(jnp.int32)
    cls_ref[0] = cls_row
    onehot = (lax.broadcasted_iota(jnp.int32, (CNT_ROWS, rows), 0) == cls_row).astype(jnp.float32)
    cnt_ref[0] = jnp.broadcast_to(jnp.sum(onehot, axis=1, keepdims=True), (CNT_ROWS, LANES)).astype(jnp.int32)


def _const_spec(shape):
    nd = len(shape)
    return pl.BlockSpec(shape, lambda i: (0,) * nd, pipeline_mode=pl.Buffered(1))


def _mixer_call(x, p, bsz, tt):
    t_tok, d = x.shape
    rows = bsz * tt
    n_tiles = t_tok // rows
    wa = p["conv_a_w"].shape[-1]
    wb = p["conv_b_w"].shape[-1]
    wc = p["pool_scale"].shape[-1]
    heads = p["lru_wa"].shape[0]
    gd = wc // len(POOL_WINDOWS)
    dims = (bsz, tt, d, wa, wb, wc, heads, p["n_groups"])
    weights = [p["norm1_g"], p["w_in"], p["conv_a_w"], p["w_a_out"], p["conv_b_w"], p["conv_b_b"],
               p["lru_wa"], p["lru_ba"], p["lru_wx"], p["lru_bx"], p["lru_lambda"], p["w_b_out"],
               p["pool_w"], p["pool_scale"], p["w_c_out"], p["w_o"], p["norm2_g"],
               p["wr_hi"], p["wr_lo"], p["b_r"]]
    f32 = jnp.float32
    return pl.pallas_call(
        functools.partial(_mixer_body, dims),
        grid=(n_tiles,),
        in_specs=[pl.BlockSpec((rows, d), lambda i: (i, 0))] + [_const_spec(w.shape) for w in weights],
        out_specs=[pl.BlockSpec((rows, d + AUG), lambda i: (i, 0)),
                   pl.BlockSpec((1, 1, rows), lambda i: (i, 0, 0)),
                   pl.BlockSpec((1, CNT_ROWS, LANES), lambda i: (i, 0, 0))],
        out_shape=[jax.ShapeDtypeStruct((t_tok, d + AUG), f32),
                   jax.ShapeDtypeStruct((n_tiles, 1, rows), jnp.int32),
                   jax.ShapeDtypeStruct((n_tiles, CNT_ROWS, LANES), jnp.int32)],
        scratch_shapes=[pltpu.VMEM(((A_CONV - 1) * bsz + rows, wa), f32),
                        pltpu.VMEM(((B_CONV - 1) * bsz + rows, wb), f32),
                        pltpu.VMEM((bsz + rows, wc), f32),
                        pltpu.VMEM((2 * bsz + rows, wc - gd), f32),
                        pltpu.VMEM((4 * bsz + rows, wc - 2 * gd), f32),
                        pltpu.VMEM((8 * bsz + rows, wc - 3 * gd), f32),
                        pltpu.VMEM((rows, wb), f32),
                        pltpu.VMEM((rows, wb), f32),
                        pltpu.VMEM((bsz, wb), f32)],
        compiler_params=pltpu.CompilerParams(dimension_semantics=("arbitrary",),
                                             vmem_limit_bytes=VMEM_LIMIT_BYTES),
        name="mixer",
    )(x, *weights)


def _sort_body(n_cls, tm, n_tiles, cls_ref, cnt_ref, src_ref, tile_ea_ref, tile_eb_ref, tile_rows_ref,
               cur_ref):
    t_tok = cls_ref.shape[0]
    n_cnt_tiles = cnt_ref.shape[0] // CNT_ROWS

    def experts(c):
        grp = c // N_PAIRS
        pair = c % N_PAIRS
        ge3 = jnp.asarray(pair >= 3, jnp.int32)
        ge5 = jnp.asarray(pair >= 5, jnp.int32)
        base = grp * EXPERTS_PER_GROUP
        return base + ge3 + ge5, base + pair + 1 - 2 * ge3 - ge5

    def class_step(c, carry):
        off, tile = carry
        n = lax.fori_loop(0, n_cnt_tiles, lambda k, acc: acc + cnt_ref[k * CNT_ROWS + c], 0)
        cur_ref[c] = off
        n_t = (n + tm - 1) // tm
        ea, eb = experts(c)

        def tile_step(k, _):
            tile_ea_ref[tile + k] = ea
            tile_eb_ref[tile + k] = eb
            tile_rows_ref[tile + k] = jnp.minimum(n - k * tm, tm)
            return 0

        lax.fori_loop(0, n_t, tile_step, 0)

        def pad_step(q, _):
            src_ref[q] = 0
            return 0

        lax.fori_loop(off + n, off + n_t * tm, pad_step, 0)
        return off + n_t * tm, tile + n_t

    _, used = lax.fori_loop(0, n_cls, class_step, (0, 0))
    base_last = ((n_cls - 1) // N_PAIRS) * EXPERTS_PER_GROUP
    ea_last, eb_last = base_last + PAIRS[-1][0], base_last + PAIRS[-1][1]

    def unused_tile(k, _):
        tile_ea_ref[k] = ea_last
        tile_eb_ref[k] = eb_last
        tile_rows_ref[k] = 0
        return 0

    lax.fori_loop(used, n_tiles, unused_tile, 0)

    def unused_slot(q, _):
        src_ref[q] = 0
        return 0

    lax.fori_loop(used * tm, n_tiles * tm, unused_slot, 0)

    def place(t, _):
        c = cls_ref[t]
        pos = cur_ref[c]
        src_ref[pos] = t
        cur_ref[c] = pos + 1
        return 0

    lax.fori_loop(0, t_tok, place, 0)


def _sort_call(cls, cnt, n_cls, tm):
    t_tok = cls.shape[0]
    n_tiles = t_tok // tm + n_cls
    smem = pl.BlockSpec(memory_space=pltpu.SMEM)
    tile_table = jax.ShapeDtypeStruct((n_tiles,), jnp.int32)
    return pl.pallas_call(
        functools.partial(_sort_body, n_cls, tm, n_tiles),
        in_specs=[smem, smem],
        out_specs=[smem, smem, smem, smem],
        out_shape=[jax.ShapeDtypeStruct((n_tiles * tm,), jnp.int32), tile_table, tile_table, tile_table],
        scratch_shapes=[pltpu.SMEM((CNT_ROWS,), jnp.int32)],
        name="sort",
    )(cls, cnt)


def _moe_body(dims, ea_ref, eb_ref, nrows_ref, src_ref,
              xa_hbm, g2_ref, gf_ref, w1a_ref, w3a_ref, w2a_ref, w1b_ref, w3b_ref, w2b_ref,
              out_hbm, xbuf, obuf, sem_in, sem_out):
    tm, d, bsz, seq, final = dims
    i = pl.program_id(0)
    nrows = nrows_ref[i]
    base = i * tm

    def dst_row(t):
        return (t % bsz) * seq + t // bsz if final else t

    @pl.when(nrows > 0)
    def _():
        def gather(j, _):
            t = src_ref[base + j]
            pltpu.make_async_copy(xa_hbm.at[pl.ds(t, 1), :], xbuf.at[pl.ds(j, 1), :], sem_in).start()
            return 0

        lax.fori_loop(0, tm, gather, 0)
        pltpu.make_async_copy(xa_hbm.at[pl.ds(0, tm), :], xbuf, sem_in).wait()

        xa = xbuf[...]
        x = xa[:, 0:d]
        g_a = xa[:, d:d + 1]
        g_b = xa[:, d + 1:d + 2]
        h = _rmsnorm(x, g2_ref[...]).astype(jnp.bfloat16)
        hid_a = (jax.nn.silu(_dot(h, w1a_ref[0])) * _dot(h, w3a_ref[0])).astype(jnp.bfloat16)
        y = g_a * _dot(hid_a, w2a_ref[0])
        hid_b = (jax.nn.silu(_dot(h, w1b_ref[0])) * _dot(h, w3b_ref[0])).astype(jnp.bfloat16)
        y = y + g_b * _dot(hid_b, w2b_ref[0])
        out = x + y
        if final:
            out = _rmsnorm(out, gf_ref[...])
        obuf[...] = out

        def scatter(j, _):
            t = src_ref[base + j]
            pltpu.make_async_copy(obuf.at[pl.ds(j, 1), :], out_hbm.at[pl.ds(dst_row(t), 1), :], sem_out).start()
            return 0

        lax.fori_loop(0, nrows, scatter, 0)
        def wait_rows(n):
            pltpu.make_async_copy(obuf.at[pl.ds(0, n), :], out_hbm.at[pl.ds(0, n), :], sem_out).wait()

        groups = pl.multiple_of((nrows // SUBLANES) * SUBLANES, SUBLANES)

        @pl.when(groups > 0)
        def _():
            wait_rows(groups)

        for k in (4, 2, 1):
            @pl.when((nrows & k) != 0)
            def _():
                wait_rows(k)


def _moe_call(xa, src, tile_ea, tile_eb, tile_rows, p, bsz, seq, tm, final):
    t_tok = xa.shape[0]
    d = xa.shape[1] - AUG
    f = p["moe_w1"].shape[-1]
    n_tiles = tile_rows.shape[0]
    dims = (tm, d, bsz, seq, final)

    def w_spec(shape, which):
        if which == 0:
            return pl.BlockSpec((1,) + shape, lambda i, ea, eb, nr, src: (ea[i], 0, 0))
        return pl.BlockSpec((1,) + shape, lambda i, ea, eb, nr, src: (eb[i], 0, 0))

    row = pl.BlockSpec((1, d), lambda i, ea, eb, nr, src: (0, 0))
    return pl.pallas_call(
        functools.partial(_moe_body, dims),
        grid_spec=pltpu.PrefetchScalarGridSpec(
            num_scalar_prefetch=4,
            grid=(n_tiles,),
            in_specs=[pl.BlockSpec(memory_space=pl.ANY), row, row,
                      w_spec((d, f), 0), w_spec((d, f), 0), w_spec((f, d), 0),
                      w_spec((d, f), 1), w_spec((d, f), 1), w_spec((f, d), 1)],
            out_specs=pl.BlockSpec(memory_space=pl.ANY),
            scratch_shapes=[pltpu.VMEM((tm, d + AUG), jnp.float32),
                            pltpu.VMEM((tm, d), jnp.float32),
                            pltpu.SemaphoreType.DMA, pltpu.SemaphoreType.DMA],
        ),
        out_shape=jax.ShapeDtypeStruct((t_tok, d), jnp.float32),
        compiler_params=pltpu.CompilerParams(dimension_semantics=("arbitrary",),
                                             vmem_limit_bytes=VMEM_LIMIT_BYTES),
        name="moe",
    )(tile_ea, tile_eb, tile_rows, src, xa, p["norm2_g"], p["final_g"],
      p["moe_w1"], p["moe_w3"], p["moe_w2"], p["moe_w1"], p["moe_w3"], p["moe_w2"])


def _layer_params(l, norm1_g, w_in, conv_a_w, w_a_out, conv_b_w, conv_b_b, lru_wa, lru_ba, lru_wx, lru_bx,
                  lru_lambda, w_b_out, pool_w, pool_scale, w_c_out, w_o, norm2_g, router_grp_w, router_grp_b,
                  router_exp_w, router_exp_b, moe_w1, moe_w3, moe_w2, final_g):
    bf16 = jnp.bfloat16
    d = w_in.shape[1]
    n_groups = router_grp_w.shape[-1]
    n_exp = router_exp_w.shape[-1]
    assert n_exp == n_groups * EXPERTS_PER_GROUP and n_groups + n_exp <= LANES
    assert n_groups * N_PAIRS <= CNT_ROWS
    w_r = jnp.concatenate([router_grp_w[l], router_exp_w[l]], axis=-1)
    w_r = jnp.pad(w_r, ((0, 0), (0, LANES - w_r.shape[-1])))
    wr_hi = w_r.astype(bf16)
    wr_lo = (w_r - wr_hi.astype(jnp.float32)).astype(bf16)
    b_r = jnp.concatenate([router_grp_b[l], router_exp_b[l]])
    b_r = jnp.pad(b_r, (0, LANES - b_r.shape[0]))[None, :]
    return dict(
        n_groups=n_groups,
        norm1_g=norm1_g[l][None, :], w_in=w_in[l].astype(bf16), conv_a_w=conv_a_w[l],
        w_a_out=w_a_out[l].astype(bf16), conv_b_w=conv_b_w[l], conv_b_b=conv_b_b[l][None, :],
        lru_wa=lru_wa[l].astype(bf16), lru_ba=lru_ba[l].reshape(1, -1),
        lru_wx=lru_wx[l].astype(bf16), lru_bx=lru_bx[l].reshape(1, -1),
        lru_lambda=lru_lambda[l][None, :], w_b_out=w_b_out[l].astype(bf16),
        pool_w=pool_w[l].astype(bf16), pool_scale=pool_scale[l][None, :],
        w_c_out=w_c_out[l].astype(bf16), w_o=w_o[l].astype(bf16), norm2_g=norm2_g[l][None, :],
        wr_hi=wr_hi, wr_lo=wr_lo, b_r=b_r,
        moe_w1=moe_w1[l].astype(bf16), moe_w3=moe_w3[l].astype(bf16), moe_w2=moe_w2[l].astype(bf16),
        final_g=final_g[None, :],
    )


def kernel(x, norm1_g, w_in, conv_a_w, w_a_out, conv_b_w, conv_b_b, lru_wa, lru_ba, lru_wx, lru_bx, lru_lambda, w_b_out, pool_w, pool_scale, w_c_out, w_o, norm2_g, router_grp_w, router_grp_b, router_exp_w, router_exp_b, moe_w1, moe_w3, moe_w2, final_g):
    bsz, seq, d = x.shape
    depth = w_in.shape[0]
    assert bsz % SUBLANES == 0
    tt = min(MIXER_TIMESTEPS, seq)
    tm = min(MOE_TILE, bsz * seq)
    assert seq % tt == 0 and (bsz * seq) % tm == 0 and tt >= max(POOL_WINDOWS) // 2
    n_groups = router_grp_w.shape[-1]
    n_cls = n_groups * N_PAIRS

    xt = jnp.transpose(x, (1, 0, 2)).reshape(seq * bsz, d)
    for l in range(depth):
        p = _layer_params(l, norm1_g, w_in, conv_a_w, w_a_out, conv_b_w, conv_b_b, lru_wa, lru_ba, lru_wx,
                          lru_bx, lru_lambda, w_b_out, pool_w, pool_scale, w_c_out, w_o, norm2_g,
                          router_grp_w, router_grp_b, router_exp_w, router_exp_b, moe_w1, moe_w3, moe_w2,
                          final_g)
        xa, cls, cnt = _mixer_call(xt, p, bsz, tt)
        src, tile_ea, tile_eb, tile_rows = _sort_call(cls.reshape(-1), cnt[:, :, 0].reshape(-1), n_cls, tm)
        xt = _moe_call(xa, src, tile_ea, tile_eb, tile_rows, p, bsz, seq, tm, final=(l == depth - 1))
    return xt.reshape(bsz, seq, d)
```

```python
import functools

import jax
import jax.numpy as jnp
from jax import lax
from jax.experimental import pallas as pl
from jax.experimental.pallas import tpu as pltpu

LANES = 128
SUBLANES = 8
VMEM_LIMIT_BYTES = 56 * 1024 * 1024

EPS = 1e-6
LRU_C = 8.0
POOL_WINDOWS = (2, 4, 8, 16)
A_CONV = 3
B_CONV = 4
EXPERTS_PER_GROUP = 4
PAIRS = ((0, 1), (0, 2), (0, 3), (1, 2), (1, 3), (2, 3))
N_PAIRS = len(PAIRS)

MIXER_TIMESTEPS = 64
MOE_TILE = 256
AUG = LANES
CNT_ROWS = 32
SORT_UNROLL = 8
DMA_UNROLL = 8
NEG_BIG = -1e30


def _dot(a, b):
    return jnp.dot(a, b, preferred_element_type=jnp.float32)


def _rmsnorm(x, g):
    return x * lax.rsqrt(jnp.mean(x * x, axis=-1, keepdims=True) + EPS) * g


def _shifted(buf, val, halo, shifts):
    rows = val.shape[0]
    buf[halo:halo + rows, :] = val
    outs = [buf[halo - s:halo - s + rows, :] for s in shifts]
    buf[0:halo, :] = buf[rows:rows + halo, :]
    return outs


def _mixer_body(dims, x_ref, g1_ref, w_in_ref, caw_ref, wao_ref, cbw_ref, cbb_ref, lwa_ref, lba_ref,
                lwx_ref, lbx_ref, lam_ref, wbo_ref, pw_ref, ps_ref, wco_ref, wo_ref, g2_ref,
                wrh_ref, wrl_ref, br_ref, tril_ref,
                xa_ref, idx_ref, cnt_ref,
                buf_a, buf_b, buf_c1, buf_c2, buf_c4, buf_c8, a_scr, b_scr, h_state, run_cnt):
    bsz, tt, d, wa, wb, wc, heads, n_groups = dims
    rows = bsz * tt
    hd = wb // heads
    gd = wc // len(POOL_WINDOWS)
    i = pl.program_id(0)

    @pl.when(i == 0)
    def _():
        buf_a[0:(A_CONV - 1) * bsz, :] = jnp.zeros(((A_CONV - 1) * bsz, wa), jnp.float32)
        buf_b[0:(B_CONV - 1) * bsz, :] = jnp.zeros(((B_CONV - 1) * bsz, wb), jnp.float32)
        buf_c1[0:bsz, :] = jnp.zeros((bsz, wc), jnp.float32)
        buf_c2[0:2 * bsz, :] = jnp.zeros((2 * bsz, wc - gd), jnp.float32)
        buf_c4[0:4 * bsz, :] = jnp.zeros((4 * bsz, wc - 2 * gd), jnp.float32)
        buf_c8[0:8 * bsz, :] = jnp.zeros((8 * bsz, wc - 3 * gd), jnp.float32)
        h_state[...] = jnp.zeros((bsz, wb), jnp.float32)
        run_cnt[...] = jnp.zeros((1, LANES), jnp.float32)

    x = x_ref[...]
    h = _rmsnorm(x, g1_ref[...]).astype(jnp.bfloat16)

    o_b = 3 * wa
    o_c = o_b + 2 * wb
    o_g = o_c + wc

    z_a = _dot(h, w_in_ref[:, 0:o_b])
    p = z_a[:, wa:2 * wa] * z_a[:, 2 * wa:3 * wa]
    p1, p2 = _shifted(buf_a, p, (A_CONV - 1) * bsz, (bsz, 2 * bsz))
    caw = caw_ref[...]
    y_a = z_a[:, 0:wa] * (caw[2:3, :] * p + caw[1:2, :] * p1 + caw[0:1, :] * p2)
    m = jax.nn.sigmoid(_dot(h, w_in_ref[:, o_g:o_g + d])) * _dot(y_a.astype(jnp.bfloat16), wao_ref[...])

    z_b = _dot(h, w_in_ref[:, o_b:o_c])
    b_x = z_b[:, 0:wb]
    x1, x2, x3 = _shifted(buf_b, b_x, (B_CONV - 1) * bsz, (bsz, 2 * bsz, 3 * bsz))
    cbw = cbw_ref[...]
    xb = cbw[3:4, :] * b_x + cbw[2:3, :] * x1 + cbw[1:2, :] * x2 + cbw[0:1, :] * x3 + cbb_ref[...]
    xb16 = xb.astype(jnp.bfloat16)
    pre_r = jnp.concatenate([_dot(xb16[:, k * hd:(k + 1) * hd], lwa_ref[k]) for k in range(heads)], axis=-1)
    pre_i = jnp.concatenate([_dot(xb16[:, k * hd:(k + 1) * hd], lwx_ref[k]) for k in range(heads)], axis=-1)
    r = jax.nn.sigmoid(pre_r + lba_ref[...])
    ig = jax.nn.sigmoid(pre_i + lbx_ref[...])
    lam = lam_ref[...]
    softplus_neg_lam = jnp.maximum(-lam, 0.0) + jnp.log1p(jnp.exp(-jnp.abs(lam)))
    log_a = (-LRU_C * softplus_neg_lam) * r
    a = jnp.exp(log_a)
    a_scr[...] = a
    b_scr[...] = jnp.sqrt(1.0 - a * a) * (ig * xb)

    def scan_step(t, carry):
        sl = pl.ds(pl.multiple_of(t * bsz, bsz), bsz)
        carry = a_scr[sl, :] * carry + b_scr[sl, :]
        b_scr[sl, :] = carry
        return carry

    h_state[...] = lax.fori_loop(0, tt, scan_step, h_state[...])
    y_b = b_scr[...] * jax.nn.gelu(z_b[:, wb:2 * wb])
    m = m + jax.nn.sigmoid(_dot(h, w_in_ref[:, o_g + d:o_g + 2 * d])) * _dot(y_b.astype(jnp.bfloat16), wbo_ref[...])

    u = _dot(h, w_in_ref[:, o_c:o_g])
    (u1,) = _shifted(buf_c1, u, bsz, (bsz,))
    s2 = u + u1
    s2r = s2[:, gd:]
    (s2s,) = _shifted(buf_c2, s2r, 2 * bsz, (2 * bsz,))
    s4 = s2r + s2s
    s4r = s4[:, gd:]
    (s4s,) = _shifted(buf_c4, s4r, 4 * bsz, (4 * bsz,))
    s8 = s4r + s4s
    s8r = s8[:, gd:]
    (s8s,) = _shifted(buf_c8, s8r, 8 * bsz, (8 * bsz,))
    s16 = s8r + s8s
    sums = (s2[:, 0:gd], s4[:, 0:gd], s8[:, 0:gd], s16)
    t_abs = (lax.broadcasted_iota(jnp.int32, (rows, gd), 0) // bsz + i * tt + 1).astype(jnp.float32)
    ps = ps_ref[...]
    y_c = []
    for g, w in enumerate(POOL_WINDOWS):
        pooled = sums[g] / jnp.minimum(t_abs, float(w))
        pc = (pooled - u[:, g * gd:(g + 1) * gd]).astype(jnp.bfloat16)
        y_c.append(_dot(pc, pw_ref[g]) * ps[:, g * gd:(g + 1) * gd])
    y_c = jnp.concatenate(y_c, axis=-1)
    m = m + jax.nn.sigmoid(_dot(h, w_in_ref[:, o_g + 2 * d:o_g + 3 * d])) * _dot(y_c.astype(jnp.bfloat16), wco_ref[...])

    x_new = x + _dot(m.astype(jnp.bfloat16), wo_ref[...])
    xa_ref[:, 0:d] = x_new

    h2 = _rmsnorm(x_new, g2_ref[...])
    h2_hi = h2.astype(jnp.bfloat16)
    h2_lo = (h2 - h2_hi.astype(jnp.float32)).astype(jnp.bfloat16)
    logits = (_dot(h2_hi, wrh_ref[...]) + _dot(h2_hi, wrl_ref[...]) + _dot(h2_lo, wrh_ref[...])) + br_ref[...]
    lane = lax.broadcasted_iota(jnp.int32, (rows, LANES), 1).astype(jnp.float32)
    is_grp = lane < n_groups
    gl = jnp.where(is_grp, logits, NEG_BIG)
    gmax = jnp.max(gl, axis=-1, keepdims=True)
    gidx = jnp.min(jnp.where(gl == gmax, lane, float(LANES)), axis=-1, keepdims=True)
    g_p = 1.0 / jnp.sum(jnp.where(is_grp, jnp.exp(gl - gmax), 0.0), axis=-1, keepdims=True)
    e_base = n_groups + EXPERTS_PER_GROUP * gidx
    el = jnp.where((lane >= e_base) & (lane < e_base + EXPERTS_PER_GROUP), logits, NEG_BIG)
    m1 = jnp.max(el, axis=-1, keepdims=True)
    i1 = jnp.min(jnp.where(el == m1, lane, float(LANES)), axis=-1, keepdims=True)
    el2 = jnp.where(lane == i1, NEG_BIG, el)
    m2 = jnp.max(el2, axis=-1, keepdims=True)
    i2 = jnp.min(jnp.where(el2 == m2, lane, float(LANES)), axis=-1, keepdims=True)
    e21 = jnp.exp(m2 - m1)
    w1 = g_p / (1.0 + e21)
    w2 = g_p * e21 / (1.0 + e21)
    first_is_lo = i1 < i2
    loc_lo = jnp.where(first_is_lo, i1, i2) - e_base
    loc_hi = jnp.where(first_is_lo, i2, i1) - e_base
    pair = jnp.where(loc_lo == 0.0, loc_hi - 1.0, jnp.where(loc_lo == 1.0, loc_hi + 1.0, float(N_PAIRS - 1)))
    cls = gidx * float(N_PAIRS) + pair
    onehot = jnp.where(lane == cls, 1.0, 0.0)
    before = _dot(tril_ref[...], onehot.astype(jnp.bfloat16)) + run_cnt[...]
    rank = jnp.sum(onehot * before, axis=-1, keepdims=True)
    run_cnt[...] = run_cnt[...] + jnp.sum(onehot, axis=0, keepdims=True)
    aug = jnp.where(lane == 0.0, jnp.where(first_is_lo, w1, w2),
                    jnp.where(lane == 1.0, jnp.where(first_is_lo, w2, w1),
                              jnp.where(lane == 2.0, cls, jnp.where(lane == 3.0, rank, 0.0))))
    xa_ref[:, d:d + AUG] = aug
    idx_ref[0] = aug.T[0:SUBLANES, :].astype(jnp.int32)
    cnt_ref[0] = jnp.broadcast_to(run_cnt[...], (SUBLANES, LANES)).astype(jnp.int32)


def _const_spec(shape):
    nd = len(shape)
    return pl.BlockSpec(shape, lambda i: (0,) * nd, pipeline_mode=pl.Buffered(1))


def _mixer_call(x, p, bsz, tt):
    t_tok, d = x.shape
    rows = bsz * tt
    n_tiles = t_tok // rows
    wa = p["conv_a_w"].shape[-1]
    wb = p["conv_b_w"].shape[-1]
    wc = p["pool_scale"].shape[-1]
    heads = p["lru_wa"].shape[0]
    gd = wc // len(POOL_WINDOWS)
    dims = (bsz, tt, d, wa, wb, wc, heads, p["n_groups"])
    weights = [p["norm1_g"], p["w_in"], p["conv_a_w"], p["w_a_out"], p["conv_b_w"], p["conv_b_b"],
               p["lru_wa"], p["lru_ba"], p["lru_wx"], p["lru_bx"], p["lru_lambda"], p["w_b_out"],
               p["pool_w"], p["pool_scale"], p["w_c_out"], p["w_o"], p["norm2_g"],
               p["wr_hi"], p["wr_lo"], p["b_r"], jnp.tri(rows, k=-1, dtype=jnp.bfloat16)]
    f32 = jnp.float32
    return pl.pallas_call(
        functools.partial(_mixer_body, dims),
        grid=(n_tiles,),
        in_specs=[pl.BlockSpec((rows, d), lambda i: (i, 0))] + [_const_spec(w.shape) for w in weights],
        out_specs=[pl.BlockSpec((rows, d + AUG), lambda i: (i, 0)),
                   pl.BlockSpec((1, SUBLANES, rows), lambda i: (i, 0, 0)),
                   pl.BlockSpec((1, SUBLANES, LANES), lambda i: (i, 0, 0))],
        out_shape=[jax.ShapeDtypeStruct((t_tok, d + AUG), f32),
                   jax.ShapeDtypeStruct((n_tiles, SUBLANES, rows), jnp.int32),
                   jax.ShapeDtypeStruct((n_tiles, SUBLANES, LANES), jnp.int32)],
        scratch_shapes=[pltpu.VMEM(((A_CONV - 1) * bsz + rows, wa), f32),
                        pltpu.VMEM(((B_CONV - 1) * bsz + rows, wb), f32),
                        pltpu.VMEM((bsz + rows, wc), f32),
                        pltpu.VMEM((2 * bsz + rows, wc - gd), f32),
                        pltpu.VMEM((4 * bsz + rows, wc - 2 * gd), f32),
                        pltpu.VMEM((8 * bsz + rows, wc - 3 * gd), f32),
                        pltpu.VMEM((rows, wb), f32),
                        pltpu.VMEM((rows, wb), f32),
                        pltpu.VMEM((bsz, wb), f32),
                        pltpu.VMEM((1, LANES), f32)],
        compiler_params=pltpu.CompilerParams(dimension_semantics=("arbitrary",),
                                             vmem_limit_bytes=VMEM_LIMIT_BYTES),
        name="mixer",
    )(x, *weights)


def _sort_body(n_cls, tm, n_tiles, cls_ref, rank_ref, cnt_ref, src_ref, tile_ea_ref, tile_eb_ref, tile_rows_ref,
               off_ref):
    t_tok = cls_ref.shape[0]

    def experts(c):
        grp = c // N_PAIRS
        pair = c % N_PAIRS
        ge3 = jnp.asarray(pair >= 3, jnp.int32)
        ge5 = jnp.asarray(pair >= 5, jnp.int32)
        base = grp * EXPERTS_PER_GROUP
        return base + ge3 + ge5, base + pair + 1 - 2 * ge3 - ge5

    def class_step(c, carry):
        off, tile = carry
        n = cnt_ref[c]
        off_ref[c] = off
        n_t = (n + tm - 1) // tm
        ea, eb = experts(c)

        def tile_step(k, _):
            tile_ea_ref[tile + k] = ea
            tile_eb_ref[tile + k] = eb
            tile_rows_ref[tile + k] = jnp.minimum(n - k * tm, tm)
            return 0

        lax.fori_loop(0, n_t, tile_step, 0)
        return off + n_t * tm, tile + n_t

    _, used = lax.fori_loop(0, n_cls, class_step, (0, 0))
    base_last = ((n_cls - 1) // N_PAIRS) * EXPERTS_PER_GROUP
    ea_last, eb_last = base_last + PAIRS[-1][0], base_last + PAIRS[-1][1]

    def unused_tile(k, _):
        tile_ea_ref[k] = ea_last
        tile_eb_ref[k] = eb_last
        tile_rows_ref[k] = 0
        return 0

    lax.fori_loop(used, n_tiles, unused_tile, 0)

    def unused_slot(q, _):
        src_ref[q] = 0
        return 0

    lax.fori_loop(used * tm, n_tiles * tm, unused_slot, 0)

    def place(k, _):
        for u in range(SORT_UNROLL):
            t = k * SORT_UNROLL + u
            src_ref[off_ref[cls_ref[t]] + rank_ref[t]] = t
        return 0

    lax.fori_loop(0, t_tok // SORT_UNROLL, place, 0)

    def pad_class(c, _):
        n = cnt_ref[c]
        off = off_ref[c]
        first = src_ref[off]

        def pad_step(q, _):
            src_ref[q] = first
            return 0

        lax.fori_loop(off + n, off + ((n + tm - 1) // tm) * tm, pad_step, 0)
        return 0

    lax.fori_loop(0, n_cls, pad_class, 0)


def _sort_call(cls, rank, cnt, n_cls, tm):
    t_tok = cls.shape[0]
    assert t_tok % SORT_UNROLL == 0
    n_tiles = t_tok // tm + n_cls
    smem = pl.BlockSpec(memory_space=pltpu.SMEM)
    tile_table = jax.ShapeDtypeStruct((n_tiles,), jnp.int32)
    return pl.pallas_call(
        functools.partial(_sort_body, n_cls, tm, n_tiles),
        in_specs=[smem, smem, smem],
        out_specs=[smem, smem, smem, smem],
        out_shape=[jax.ShapeDtypeStruct((n_tiles * tm,), jnp.int32), tile_table, tile_table, tile_table],
        scratch_shapes=[pltpu.SMEM((CNT_ROWS,), jnp.int32)],
        name="sort",
    )(cls, rank, cnt)


def _moe_body(dims, ea_ref, eb_ref, nrows_ref, src_ref,
              xa_hbm, g2_ref, gf_ref, w1a_ref, w3a_ref, w2a_ref, w1b_ref, w3b_ref, w2b_ref,
              out_hbm, xbuf, obuf, sem_in, sem_out):
    tm, d, bsz, seq, final = dims
    i = pl.program_id(0)
    nrows = nrows_ref[i]
    base = i * tm

    def dst_row(t):
        return (t % bsz) * seq + t // bsz if final else t

    n_steps = pl.num_programs(0)
    slot = i % 2
    nxt = jnp.minimum(i + 1, n_steps - 1)
    has_next = (i + 1 < n_steps) & (nrows_ref[nxt] > 0)

    def start_gather(tile, to_slot):
        def body(k, _):
            for u in range(DMA_UNROLL):
                j = k * DMA_UNROLL + u
                t = src_ref[tile * tm + j]
                pltpu.make_async_copy(xa_hbm.at[pl.ds(t, 1), :], xbuf.at[to_slot, pl.ds(j, 1), :],
                                      sem_in.at[to_slot]).start()
            return 0

        lax.fori_loop(0, tm // DMA_UNROLL, body, 0)

    def wait_scatter(of_slot, n):
        def wait_rows(k):
            pltpu.make_async_copy(obuf.at[of_slot, pl.ds(0, k), :], out_hbm.at[pl.ds(0, k), :],
                                  sem_out.at[of_slot]).wait()

        groups = pl.multiple_of((n // SUBLANES) * SUBLANES, SUBLANES)

        @pl.when(groups > 0)
        def _():
            wait_rows(groups)

        for k in (4, 2, 1):
            @pl.when((n & k) != 0)
            def _():
                wait_rows(k)

    @pl.when(i == 0)
    def _():
        start_gather(0, 0)

    @pl.when(nrows > 0)
    def _():
        pltpu.make_async_copy(xa_hbm.at[pl.ds(0, tm), :], xbuf.at[slot], sem_in.at[slot]).wait()

        @pl.when(has_next)
        def _():
            start_gather(i + 1, 1 - slot)

        xa = xbuf[slot]
        x = xa[:, 0:d]
        g_a = xa[:, d:d + 1]
        g_b = xa[:, d + 1:d + 2]
        h = _rmsnorm(x, g2_ref[...]).astype(jnp.bfloat16)
        hid_a = (jax.nn.silu(_dot(h, w1a_ref[0])) * _dot(h, w3a_ref[0])).astype(jnp.bfloat16)
        y = g_a * _dot(hid_a, w2a_ref[0])
        hid_b = (jax.nn.silu(_dot(h, w1b_ref[0])) * _dot(h, w3b_ref[0])).astype(jnp.bfloat16)
        y = y + g_b * _dot(hid_b, w2b_ref[0])
        out = x + y
        if final:
            out = _rmsnorm(out, gf_ref[...])
        obuf[slot] = out

        @pl.when(i > 0)
        def _():
            wait_scatter(1 - slot, nrows_ref[jnp.maximum(i - 1, 0)])

        def scatter_row(j):
            t = src_ref[base + j]
            pltpu.make_async_copy(obuf.at[slot, pl.ds(j, 1), :], out_hbm.at[pl.ds(dst_row(t), 1), :],
                                  sem_out.at[slot]).start()

        def scatter_group(k, _):
            for u in range(DMA_UNROLL):
                scatter_row(k * DMA_UNROLL + u)
            return 0

        full = nrows // DMA_UNROLL
        lax.fori_loop(0, full, scatter_group, 0)
        done = full * DMA_UNROLL
        assert DMA_UNROLL == 8
        for k in (4, 2, 1):
            @pl.when((nrows & k) != 0)
            def _():
                start = done + (nrows & (DMA_UNROLL - 2 * k))
                for u in range(k):
                    scatter_row(start + u)

        @pl.when(jnp.logical_not(has_next))
        def _():
            wait_scatter(slot, nrows)


def _moe_call(xa, src, tile_ea, tile_eb, tile_rows, p, bsz, seq, tm, final):
    t_tok = xa.shape[0]
    d = xa.shape[1] - AUG
    f = p["moe_w1"].shape[-1]
    n_tiles = tile_rows.shape[0]
    dims = (tm, d, bsz, seq, final)

    def w_spec(shape, which):
        if which == 0:
            return pl.BlockSpec((1,) + shape, lambda i, ea, eb, nr, src: (ea[i], 0, 0))
        return pl.BlockSpec((1,) + shape, lambda i, ea, eb, nr, src: (eb[i], 0, 0))

    row = pl.BlockSpec((1, d), lambda i, ea, eb, nr, src: (0, 0))
    return pl.pallas_call(
        functools.partial(_moe_body, dims),
        grid_spec=pltpu.PrefetchScalarGridSpec(
            num_scalar_prefetch=4,
            grid=(n_tiles,),
            in_specs=[pl.BlockSpec(memory_space=pl.ANY), row, row,
                      w_spec((d, f), 0), w_spec((d, f), 0), w_spec((f, d), 0),
                      w_spec((d, f), 1), w_spec((d, f), 1), w_spec((f, d), 1)],
            out_specs=pl.BlockSpec(memory_space=pl.ANY),
            scratch_shapes=[pltpu.VMEM((2, tm, d + AUG), jnp.float32),
                            pltpu.VMEM((2, tm, d), jnp.float32),
                            pltpu.SemaphoreType.DMA((2,)), pltpu.SemaphoreType.DMA((2,))],
        ),
        out_shape=jax.ShapeDtypeStruct((t_tok, d), jnp.float32),
        compiler_params=pltpu.CompilerParams(dimension_semantics=("arbitrary",),
                                             vmem_limit_bytes=VMEM_LIMIT_BYTES),
        name="moe",
    )(tile_ea, tile_eb, tile_rows, src, xa, p["norm2_g"], p["final_g"],
      p["moe_w1"], p["moe_w3"], p["moe_w2"], p["moe_w1"], p["moe_w3"], p["moe_w2"])


def _layer_params(l, norm1_g, w_in, conv_a_w, w_a_out, conv_b_w, conv_b_b, lru_wa, lru_ba, lru_wx, lru_bx,
                  lru_lambda, w_b_out, pool_w, pool_scale, w_c_out, w_o, norm2_g, router_grp_w, router_grp_b,
                  router_exp_w, router_exp_b, moe_w1, moe_w3, moe_w2, final_g):
    bf16 = jnp.bfloat16
    d = w_in.shape[1]
    n_groups = router_grp_w.shape[-1]
    n_exp = router_exp_w.shape[-1]
    assert n_exp == n_groups * EXPERTS_PER_GROUP and n_groups + n_exp <= LANES
    assert n_groups * N_PAIRS <= CNT_ROWS
    w_r = jnp.concatenate([router_grp_w[l], router_exp_w[l]], axis=-1)
    w_r = jnp.pad(w_r, ((0, 0), (0, LANES - w_r.shape[-1])))
    wr_hi = w_r.astype(bf16)
    wr_lo = (w_r - wr_hi.astype(jnp.float32)).astype(bf16)
    b_r = jnp.concatenate([router_grp_b[l], router_exp_b[l]])
    b_r = jnp.pad(b_r, (0, LANES - b_r.shape[0]))[None, :]
    return dict(
        n_groups=n_groups,
        norm1_g=norm1_g[l][None, :], w_in=w_in[l].astype(bf16), conv_a_w=conv_a_w[l],
        w_a_out=w_a_out[l].astype(bf16), conv_b_w=conv_b_w[l], conv_b_b=conv_b_b[l][None, :],
        lru_wa=lru_wa[l].astype(bf16), lru_ba=lru_ba[l].reshape(1, -1),
        lru_wx=lru_wx[l].astype(bf16), lru_bx=lru_bx[l].reshape(1, -1),
        lru_lambda=lru_lambda[l][None, :], w_b_out=w_b_out[l].astype(bf16),
        pool_w=pool_w[l].astype(bf16), pool_scale=pool_scale[l][None, :],
        w_c_out=w_c_out[l].astype(bf16), w_o=w_o[l].astype(bf16), norm2_g=norm2_g[l][None, :],
        wr_hi=wr_hi, wr_lo=wr_lo, b_r=b_r,
        moe_w1=moe_w1[l].astype(bf16), moe_w3=moe_w3[l].astype(bf16), moe_w2=moe_w2[l].astype(bf16),
        final_g=final_g[None, :],
    )


def kernel(x, norm1_g, w_in, conv_a_w, w_a_out, conv_b_w, conv_b_b, lru_wa, lru_ba, lru_wx, lru_bx, lru_lambda, w_b_out, pool_w, pool_scale, w_c_out, w_o, norm2_g, router_grp_w, router_grp_b, router_exp_w, router_exp_b, moe_w1, moe_w3, moe_w2, final_g):
    bsz, seq, d = x.shape
    depth = w_in.shape[0]
    assert bsz % SUBLANES == 0
    tt = min(MIXER_TIMESTEPS, seq)
    tm = min(MOE_TILE, bsz * seq)
    assert seq % tt == 0 and (bsz * seq) % tm == 0 and tt >= max(POOL_WINDOWS) // 2
    n_groups = router_grp_w.shape[-1]
    n_cls = n_groups * N_PAIRS

    xt = jnp.transpose(x, (1, 0, 2)).reshape(seq * bsz, d)
    for l in range(depth):
        p = _layer_params(l, norm1_g, w_in, conv_a_w, w_a_out, conv_b_w, conv_b_b, lru_wa, lru_ba, lru_wx,
                          lru_bx, lru_lambda, w_b_out, pool_w, pool_scale, w_c_out, w_o, norm2_g,
                          router_grp_w, router_grp_b, router_exp_w, router_exp_b, moe_w1, moe_w3, moe_w2,
                          final_g)
        xa, idx, cnt = _mixer_call(xt, p, bsz, tt)
        src, tile_ea, tile_eb, tile_rows = _sort_call(idx[:, 2, :].reshape(-1), idx[:, 3, :].reshape(-1),
                                                      cnt[-1, 0, :], n_cls, tm)
        xt = _moe_call(xa, src, tile_ea, tile_eb, tile_rows, p, bsz, seq, tm, final=(l == depth - 1))
    return xt.reshape(bsz, seq, d)
```
